```python
import jax, jax.numpy as jnp
from jax import lax
import numpy as np

D_MODEL = 1024
BATCH = 2
SEQ = 8192
DEPTH = 1

CHUNK = 64
Q_BLOCK = 128
EPS = 1e-6
N_MEM = 256
MLA_HEADS = 8
MLA_Q_RANK = 256
MLA_KV_RANK = 128
MLA_NOPE_DIM = 64
MLA_ROPE_DIM = 32
MLA_V_DIM = 64
MLA_QK_DIM = MLA_NOPE_DIM + MLA_ROPE_DIM
MLA_WIDTH = MLA_HEADS * MLA_V_DIM
ROPE_THETA = 10000.0
CONV_WIDTH = D_MODEL - MLA_WIDTH
CONV_KERNEL = 31
IN_SPLITS = [MLA_Q_RANK, MLA_Q_RANK + MLA_KV_RANK, MLA_Q_RANK + MLA_KV_RANK + MLA_ROPE_DIM]
IN_COLS = MLA_Q_RANK + MLA_KV_RANK + MLA_ROPE_DIM + 2 * CONV_WIDTH
MEM_HEADS = 4
MEM_HEAD_DIM = D_MODEL // MEM_HEADS
PEER_HEADS = 8
PEER_N_KEYS = 128
PEER_N_EXPERTS = PEER_N_KEYS * PEER_N_KEYS
PEER_KEY_DIM = 256
PEER_HALF_DIM = PEER_KEY_DIM // 2
PEER_TOPK = 16
PEER_TOKEN_BLOCK = 128

kernel_name = 'hybrid_mla_conformer_peer_block'


def rms_norm(x, g):
    xf = x.astype(jnp.float32)
    y = xf * lax.rsqrt(jnp.mean(xf * xf, axis=-1, keepdims=True) + EPS)
    return (y * g.astype(jnp.float32)).astype(x.dtype)


def layer_norm(x, g, b):
    xf = x.astype(jnp.float32)
    mu = jnp.mean(xf, axis=-1, keepdims=True)
    xc = xf - mu
    var = jnp.mean(xc * xc, axis=-1, keepdims=True)
    y = xc * lax.rsqrt(var + EPS) * g.astype(jnp.float32) + b.astype(jnp.float32)
    return y.astype(x.dtype)


def rope_tables(seq_len, dim):
    inv = 1.0 / (ROPE_THETA ** (jnp.arange(0, dim, 2, dtype=jnp.float32) / dim))
    ang = jnp.arange(seq_len, dtype=jnp.float32)[:, None] * inv[None, :]
    return jnp.cos(ang), jnp.sin(ang)


def apply_rope(x, cos, sin):
    x1, x2 = jnp.split(x, 2, axis=-1)
    c = cos.astype(x.dtype)
    s = sin.astype(x.dtype)
    return jnp.concatenate([x1 * c - x2 * s, x1 * s + x2 * c], axis=-1)


def chunk_causal_attention(q, k, v):
    B, S, H, Dqk = q.shape
    nb = S // Q_BLOCK
    scale = Dqk ** -0.5
    qb = q.reshape(B, nb, Q_BLOCK, H, Dqk).transpose(1, 0, 2, 3, 4)
    k_chunk = jnp.arange(S) // CHUNK

    def one_block(args):
        blk, q_blk = args
        s = jnp.einsum('bqhd,bkhd->bhqk', q_blk, k).astype(jnp.float32) * scale
        q_chunk = (blk * Q_BLOCK + jnp.arange(Q_BLOCK)) // CHUNK
        allowed = k_chunk[None, :] <= q_chunk[:, None]
        s = jnp.where(allowed, s, -1e30)
        p = jax.nn.softmax(s, axis=-1).astype(v.dtype)
        return jnp.einsum('bhqk,bkhd->bqhd', p, v)

    out = lax.map(one_block, (jnp.arange(nb), qb))
    return out.transpose(1, 0, 2, 3, 4).reshape(B, S, H, v.shape[-1])


def hybrid_mixer(xn, w_in, q_latent_g, w_q_up, kv_latent_g, w_kv_up,
                 conv_w, conv_b, conv_ln_g, conv_ln_b, w_mix_out):
    B, S, _ = xn.shape
    proj = xn @ w_in
    c_q, c_kv, k_r, conv_in = jnp.split(proj, IN_SPLITS, axis=-1)

    q = (rms_norm(c_q, q_latent_g) @ w_q_up).reshape(B, S, MLA_HEADS, MLA_QK_DIM)
    q_nope, q_rope = q[..., :MLA_NOPE_DIM], q[..., MLA_NOPE_DIM:]
    kv = (rms_norm(c_kv, kv_latent_g) @ w_kv_up).reshape(B, S, MLA_HEADS, MLA_NOPE_DIM + MLA_V_DIM)
    k_nope, v = kv[..., :MLA_NOPE_DIM], kv[..., MLA_NOPE_DIM:]
    cos, sin = rope_tables(S, MLA_ROPE_DIM)
    q_rope = apply_rope(q_rope, cos[:, None, :], sin[:, None, :])
    k_rope = apply_rope(k_r, cos, sin)
    q_full = jnp.concatenate([q_nope, q_rope], axis=-1)
    k_full = jnp.concatenate(
        [k_nope, jnp.broadcast_to(k_rope[:, :, None, :], (B, S, MLA_HEADS, MLA_ROPE_DIM))], axis=-1)
    attn = chunk_causal_attention(q_full, k_full, v).reshape(B, S, MLA_WIDTH)

    val, gate = jnp.split(conv_in, 2, axis=-1)
    u = val * jax.nn.sigmoid(gate)
    u = jnp.pad(u, ((0, 0), (CONV_KERNEL - 1, 0), (0, 0)))
    u = lax.conv_general_dilated(u, conv_w.astype(u.dtype), window_strides=(1,), padding='VALID',
                                 dimension_numbers=('NWC', 'WIO', 'NWC'),
                                 feature_group_count=CONV_WIDTH) + conv_b
    u = jax.nn.silu(layer_norm(u, conv_ln_g, conv_ln_b))

    return jnp.concatenate([attn, u], axis=-1) @ w_mix_out


def memory_cross_attention(hn, mem, mem_kv_norm_g, w_mem_q, w_mem_kv, w_mem_out):
    B, S, _ = hn.shape
    M = mem.shape[1]
    q = (hn @ w_mem_q).reshape(B, S, MEM_HEADS, MEM_HEAD_DIM)
    kv = rms_norm(mem, mem_kv_norm_g) @ w_mem_kv
    k, v = jnp.split(kv, 2, axis=-1)
    k = k.reshape(B, M, MEM_HEADS, MEM_HEAD_DIM)
    v = v.reshape(B, M, MEM_HEADS, MEM_HEAD_DIM)
    s = jnp.einsum('bshd,bmhd->bhsm', q, k).astype(jnp.float32) * (MEM_HEAD_DIM ** -0.5)
    p = jax.nn.softmax(s, axis=-1).astype(v.dtype)
    o = jnp.einsum('bhsm,bmhd->bshd', p, v).reshape(B, S, D_MODEL)
    return o @ w_mem_out


def peer_channel_mixer(hn, w_peer_query, peer_sub_keys, peer_u, peer_v):
    B, S, D = hn.shape
    q = (hn @ w_peer_query).reshape(B, S, PEER_HEADS, 2, PEER_HALF_DIM)
    sub_scores = jnp.einsum('bshpd,hpnd->bshpn', q, peer_sub_keys).astype(jnp.float32)
    top_s, top_i = lax.top_k(sub_scores, PEER_TOPK)
    cand_s = top_s[..., 0, :, None] + top_s[..., 1, None, :]
    cand_i = top_i[..., 0, :, None] * PEER_N_KEYS + top_i[..., 1, None, :]
    cand_s = cand_s.reshape(B, S, PEER_HEADS, PEER_TOPK * PEER_TOPK)
    cand_i = cand_i.reshape(B, S, PEER_HEADS, PEER_TOPK * PEER_TOPK)
    best_s, best_pos = lax.top_k(cand_s, PEER_TOPK)
    expert_idx = jnp.take_along_axis(cand_i, best_pos, axis=-1)
    gates = jax.nn.softmax(best_s, axis=-1)

    T = B * S
    nb = T // PEER_TOKEN_BLOCK
    xt = hn.reshape(nb, PEER_TOKEN_BLOCK, D)
    it = expert_idx.reshape(nb, PEER_TOKEN_BLOCK, PEER_HEADS, PEER_TOPK)
    gt = gates.reshape(nb, PEER_TOKEN_BLOCK, PEER_HEADS, PEER_TOPK)

    def one_block(args):
        xb, ib, gb = args
        u = peer_u[ib]
        a = jax.nn.gelu(jnp.einsum('thkd,td->thk', u, xb), approximate=False)
        w = (a.astype(jnp.float32) * gb).astype(xb.dtype)
        v = peer_v[ib]
        return jnp.einsum('thk,thkd->td', w, v)

    return lax.map(one_block, (xt, it, gt)).reshape(B, S, D)


def _normal(key, shape, scale):
    return jax.random.normal(key, shape, dtype=jnp.float32) * scale


def _gain(key, shape):
    return 1.0 + 0.01 * jax.random.normal(key, shape, dtype=jnp.float32)


def setup_inputs(seed: int = 0) -> dict:
    key = jax.random.key(seed)
    ks = jax.random.split(key, 24)
    L = DEPTH
    return {
        'x': _normal(ks[0], (BATCH, SEQ, D_MODEL), 1.0),
        'mem': _normal(ks[1], (BATCH, N_MEM, D_MODEL), 1.0),
        'norm_mix_g': _gain(ks[2], (L, D_MODEL)),
        'w_in': _normal(ks[3], (L, D_MODEL, IN_COLS), D_MODEL ** -0.5),
        'q_latent_g': _gain(ks[4], (L, MLA_Q_RANK)),
        'w_q_up': _normal(ks[5], (L, MLA_Q_RANK, MLA_HEADS * MLA_QK_DIM), MLA_Q_RANK ** -0.5),
        'kv_latent_g': _gain(ks[6], (L, MLA_KV_RANK)),
        'w_kv_up': _normal(ks[7], (L, MLA_KV_RANK, MLA_HEADS * (MLA_NOPE_DIM + MLA_V_DIM)), MLA_KV_RANK ** -0.5),
        'conv_w': _normal(ks[8], (L, CONV_KERNEL, 1, CONV_WIDTH), CONV_KERNEL ** -0.5),
        'conv_b': _normal(ks[9], (L, CONV_WIDTH), 0.01),
        'conv_ln_g': _gain(ks[10], (L, CONV_WIDTH)),
        'conv_ln_b': _normal(ks[11], (L, CONV_WIDTH), 0.01),
        'w_mix_out': _normal(ks[12], (L, D_MODEL, D_MODEL), D_MODEL ** -0.5),
        'norm_mem_g': _gain(ks[13], (L, D_MODEL)),
        'mem_kv_norm_g': _gain(ks[14], (L, D_MODEL)),
        'w_mem_q': _normal(ks[15], (L, D_MODEL, D_MODEL), D_MODEL ** -0.5),
        'w_mem_kv': _normal(ks[16], (L, D_MODEL, 2 * D_MODEL), D_MODEL ** -0.5),
        'w_mem_out': _normal(ks[17], (L, D_MODEL, D_MODEL), D_MODEL ** -0.5),
        'norm_ffn_g': _gain(ks[18], (L, D_MODEL)),
        'w_peer_query': _normal(ks[19], (L, D_MODEL, PEER_HEADS * PEER_KEY_DIM), D_MODEL ** -0.5),
        'peer_sub_keys': _normal(ks[20], (L, PEER_HEADS, 2, PEER_N_KEYS, PEER_HALF_DIM), PEER_HALF_DIM ** -0.5),
        'peer_u': _normal(ks[21], (L, PEER_N_EXPERTS, D_MODEL), D_MODEL ** -0.5),
        'peer_v': _normal(ks[22], (L, PEER_N_EXPERTS, D_MODEL), 0.5),
        'final_norm_g': _gain(ks[23], (D_MODEL,)),
    }


def reference(x, mem, norm_mix_g, w_in, q_latent_g, w_q_up, kv_latent_g, w_kv_up,
              conv_w, conv_b, conv_ln_g, conv_ln_b, w_mix_out,
              norm_mem_g, mem_kv_norm_g, w_mem_q, w_mem_kv, w_mem_out,
              norm_ffn_g, w_peer_query, peer_sub_keys, peer_u, peer_v, final_norm_g):
    h = x
    for l in range(DEPTH):
        h = h + hybrid_mixer(rms_norm(h, norm_mix_g[l]), w_in[l], q_latent_g[l], w_q_up[l],
                             kv_latent_g[l], w_kv_up[l], conv_w[l], conv_b[l],
                             conv_ln_g[l], conv_ln_b[l], w_mix_out[l])
        h = h + memory_cross_attention(rms_norm(h, norm_mem_g[l]), mem, mem_kv_norm_g[l],
                                       w_mem_q[l], w_mem_kv[l], w_mem_out[l])
        h = h + peer_channel_mixer(rms_norm(h, norm_ffn_g[l]), w_peer_query[l],
                                   peer_sub_keys[l], peer_u[l], peer_v[l])
    return rms_norm(h, final_norm_g)
```

```python
import functools

import jax
import jax.numpy as jnp
from jax import lax
from jax.experimental import pallas as pl
from jax.experimental.pallas import tpu as pltpu

F32 = jnp.float32
BF16 = jnp.bfloat16

EPS = 1e-6
CHUNK = 64
LANES = 128
N_HEADS = 8
NOPE = 64
ROPE = 32
QK_DIM = NOPE + ROPE
V_DIM = 64
Q_RANK = 256
KV_RANK = 128
CONV_W = 512
CONV_K = 31
CONV_HALO = 32
MEM_HEADS = 4
MEM_DIM = 256
N_KEYS = 128
TOPK = 16
STAIR_N = tuple(TOPK // (k + 1) for k in range(TOPK))
STAIR_OFF = tuple(sum(STAIR_N[:k]) for k in range(TOPK))
STAIR_ROWS = 56
NEG_INF = float("-inf")
VMEM_LIMIT = 48 * 1024 * 1024

NT_DIMS = (((1,), (1,)), ((), ()))


def _rms(x, g):
    return x * lax.rsqrt(jnp.mean(x * x, axis=-1, keepdims=True) + EPS) * g


def _params(*sem):
    return pltpu.CompilerParams(dimension_semantics=sem, vmem_limit_bytes=VMEM_LIMIT)


def _rope(t, c, sa, sb):
    width = t.shape[1]
    reps = width // LANES
    c, sa, sb = (jnp.tile(a, (1, reps)) for a in (c, sa, sb))
    return t * c + pltpu.roll(t, width - ROPE // 2, 1) * sa + pltpu.roll(t, ROPE // 2, 1) * sb


def _mixer_in_kernel(x_ref, g_ref, win_ref, qg_ref, wq_ref, kvg_ref, wkv_ref,
                     qc_ref, qsa_ref, qsb_ref, kc_ref, ksa_ref, ksb_ref,
                     cw_ref, cb_ref, lng_ref, lnb_ref,
                     qp_ref, kp_ref, vp_ref, uc_ref, ubuf_ref, *, ts, tiles_per_seq):
    i = pl.program_id(0)
    xn = _rms(x_ref[...], g_ref[...])
    proj = jnp.dot(xn.astype(BF16), win_ref[...], preferred_element_type=F32)
    cq = proj[:, :Q_RANK]
    ckv = proj[:, Q_RANK:Q_RANK + KV_RANK]
    kr = proj[:, 384:512]
    q = jnp.dot(_rms(cq, qg_ref[...]).astype(BF16), wq_ref[...], preferred_element_type=F32)
    kv = jnp.dot(_rms(ckv, kvg_ref[...]).astype(BF16), wkv_ref[...], preferred_element_type=F32)
    width = N_HEADS * LANES
    k = kv[:, :width] + jnp.tile(pltpu.roll(kr, NOPE, 1), (1, N_HEADS))
    qp_ref[...] = _rope(q, qc_ref[...], qsa_ref[...], qsb_ref[...]).astype(qp_ref.dtype)
    kp_ref[...] = _rope(k, kc_ref[...], ksa_ref[...], ksb_ref[...]).astype(kp_ref.dtype)
    vp_ref[...] = kv[:, width:].astype(vp_ref.dtype)

    u = proj[:, 512:512 + CONV_W] * jax.nn.sigmoid(proj[:, 512 + CONV_W:])

    @pl.when(i % tiles_per_seq == 0)
    def _():
        ubuf_ref[0:CONV_HALO, :] = jnp.zeros((CONV_HALO, CONV_W), F32)

    ubuf_ref[CONV_HALO:CONV_HALO + ts, :] = u
    acc = jnp.zeros((ts, CONV_W), F32) + cb_ref[...]
    first = CONV_HALO - (CONV_K - 1)
    for tap in range(CONV_K):
        acc = acc + ubuf_ref[pl.ds(first + tap, ts), :] * cw_ref[tap:tap + 1, :]
    ubuf_ref[0:CONV_HALO, :] = ubuf_ref[ts:ts + CONV_HALO, :]
    mu = jnp.mean(acc, axis=-1, keepdims=True)
    xc = acc - mu
    var = jnp.mean(xc * xc, axis=-1, keepdims=True)
    y = xc * lax.rsqrt(var + EPS) * lng_ref[...] + lnb_ref[...]
    uc_ref[...] = (y * jax.nn.sigmoid(y)).astype(uc_ref.dtype)


def _mixer_in(x2, g, win, qg, wq, kvg, wkv, tables, cw, cb, lng, lnb, *, seq, ts):
    t_total, d = x2.shape
    tiles_per_seq = seq // ts
    width = N_HEADS * LANES
    full = lambda a: pl.BlockSpec(a.shape, lambda i: (0,) * a.ndim)
    tab = pl.BlockSpec((ts, LANES), lambda i: (i % tiles_per_seq, 0))
    row = lambda w: pl.BlockSpec((ts, w), lambda i: (i, 0))
    return pl.pallas_call(
        functools.partial(_mixer_in_kernel, ts=ts, tiles_per_seq=tiles_per_seq),
        grid=(t_total // ts,),
        in_specs=[row(d), full(g), full(win), full(qg), full(wq), full(kvg), full(wkv)]
                 + [tab] * 6 + [full(cw), full(cb), full(lng), full(lnb)],
        out_specs=[row(width), row(width), row(width), row(CONV_W)],
        out_shape=[jax.ShapeDtypeStruct((t_total, width), BF16)] * 3
                  + [jax.ShapeDtypeStruct((t_total, CONV_W), BF16)],
        scratch_shapes=[pltpu.VMEM((ts + CONV_HALO, CONV_W), F32)],
        compiler_params=_params("arbitrary"),
        name="mixer_in",
    )(x2, g, win, qg, wq, kvg, wkv, *tables, cw, cb, lng, lnb)


def _attn_kernel(qi_ref, ki_ref, q_ref, k_ref, v_ref, o_ref, m_ref, l_ref, acc_ref, *, tq, tk):
    p = pl.program_id(1)
    qi = qi_ref[p]
    ki = ki_ref[p]

    @pl.when(ki == 0)
    def _():
        m_ref[...] = jnp.full(m_ref.shape, NEG_INF, F32)
        l_ref[...] = jnp.zeros(l_ref.shape, F32)
        acc_ref[...] = jnp.zeros(acc_ref.shape, F32)

    rows = qi * tq + lax.broadcasted_iota(jnp.int32, (tq, tk), 0)
    cols = ki * tk + lax.broadcasted_iota(jnp.int32, (tq, tk), 1)
    allowed = (cols // CHUNK) <= (rows // CHUNK)
    for h in range(N_HEADS):
        sl = slice(h * LANES, (h + 1) * LANES)
        s = lax.dot_general(q_ref[:, sl], k_ref[:, sl], NT_DIMS, preferred_element_type=F32)
        s = jnp.where(allowed, s, -1e30)
        m_prev = m_ref[h]
        m_new = jnp.maximum(m_prev, jnp.max(s, axis=1, keepdims=True))
        alpha = jnp.exp(m_prev - m_new)
        e = jnp.exp(s - m_new)
        l_ref[h] = alpha * l_ref[h] + jnp.sum(e, axis=1, keepdims=True)
        acc_ref[h] = alpha * acc_ref[h] + jnp.dot(e.astype(BF16), v_ref[:, sl],
                                                  preferred_element_type=F32)
        m_ref[h] = m_new

    @pl.when(ki == qi)
    def _():
        for pair in range(N_HEADS // 2):
            a, b = 2 * pair, 2 * pair + 1
            o = acc_ref[a] / l_ref[a] + acc_ref[b] / l_ref[b]
            o_ref[:, pair * LANES:(pair + 1) * LANES] = o.astype(o_ref.dtype)


def _attention(qp, kp, vp, *, tq, tk):
    b, s, width = qp.shape
    assert tq == tk and tq % CHUNK == 0 and s % tq == 0
    nq = s // tq
    pairs = [(qi, ki) for qi in range(nq) for ki in range(qi + 1)]
    qi_arr = jnp.asarray([p[0] for p in pairs], jnp.int32)
    ki_arr = jnp.asarray([p[1] for p in pairs], jnp.int32)
    grid_spec = pltpu.PrefetchScalarGridSpec(
        num_scalar_prefetch=2,
        grid=(b, len(pairs)),
        in_specs=[
            pl.BlockSpec((None, tq, width), lambda bi, p, qi, ki: (bi, qi[p], 0)),
            pl.BlockSpec((None, tk, width), lambda bi, p, qi, ki: (bi, ki[p], 0)),
            pl.BlockSpec((None, tk, width), lambda bi, p, qi, ki: (bi, ki[p], 0)),
        ],
        out_specs=pl.BlockSpec((None, tq, N_HEADS * V_DIM), lambda bi, p, qi, ki: (bi, qi[p], 0)),
        scratch_shapes=[pltpu.VMEM((N_HEADS, tq, 1), F32), pltpu.VMEM((N_HEADS, tq, 1), F32),
                        pltpu.VMEM((N_HEADS, tq, LANES), F32)],
    )
    return pl.pallas_call(
        functools.partial(_attn_kernel, tq=tq, tk=tk),
        grid_spec=grid_spec,
        out_shape=jax.ShapeDtypeStruct((b, s, N_HEADS * V_DIM), BF16),
        compiler_params=_params("arbitrary", "arbitrary"),
        name="attention",
    )(qi_arr, ki_arr, qp, kp, vp)


def _mem_kv_kernel(mem_ref, g_ref, w_ref, k_ref, v_ref):
    d = mem_ref.shape[-1]
    kv = jnp.dot(_rms(mem_ref[...], g_ref[...]).astype(BF16), w_ref[...],
                 preferred_element_type=F32)
    k_ref[...] = kv[:, :d].astype(k_ref.dtype)
    v_ref[...] = kv[:, d:].astype(v_ref.dtype)


def _mem_kv(mem, g, w):
    b, m, d = mem.shape
    blk = pl.BlockSpec((None, m, d), lambda i: (i, 0, 0))
    return pl.pallas_call(
        _mem_kv_kernel,
        grid=(b,),
        in_specs=[blk, pl.BlockSpec(g.shape, lambda i: (0, 0)), pl.BlockSpec(w.shape, lambda i: (0, 0))],
        out_specs=[blk, blk],
        out_shape=[jax.ShapeDtypeStruct((b, m, d), BF16)] * 2,
        compiler_params=_params("arbitrary"),
        name="mem_kv",
    )(mem, g, w)


def _mid_kernel(x_ref, attn_ref, uc_ref, wo_ref, gm_ref, wmq_ref, km_ref, vm_ref, wmo_ref,
                gf_ref, wpq_ref, h2_ref, hn_ref, qp_ref):
    half = attn_ref.shape[1]
    h1 = (x_ref[...]
          + jnp.dot(attn_ref[...], wo_ref[0:half, :], preferred_element_type=F32)
          + jnp.dot(uc_ref[...], wo_ref[half:, :], preferred_element_type=F32))
    hn = _rms(h1, gm_ref[...]).astype(BF16)
    q = (jnp.dot(hn, wmq_ref[...], preferred_element_type=F32) * (MEM_DIM ** -0.5)).astype(BF16)
    outs = []
    for h in range(MEM_HEADS):
        sl = slice(h * MEM_DIM, (h + 1) * MEM_DIM)
        s = lax.dot_general(q[:, sl], km_ref[:, sl], NT_DIMS, preferred_element_type=F32)
        e = jnp.exp(s - jnp.max(s, axis=1, keepdims=True))
        o = jnp.dot(e.astype(BF16), vm_ref[:, sl], preferred_element_type=F32)
        outs.append(o / jnp.sum(e, axis=1, keepdims=True))
    o = jnp.concatenate(outs, axis=1).astype(BF16)
    h2 = h1 + jnp.dot(o, wmo_ref[...], preferred_element_type=F32)
    h2_ref[...] = h2
    hn3 = _rms(h2, gf_ref[...]).astype(BF16)
    hn_ref[...] = hn3
    qp_ref[...] = jnp.dot(hn3, wpq_ref[...], preferred_element_type=F32).astype(qp_ref.dtype)


def _mid(x2, attn, uc, wo, gm, wmq, kmem, vmem, wmo, gf, wpq, *, seq, ts):
    t_total, d = x2.shape
    tiles_per_seq = seq // ts
    full = lambda a: pl.BlockSpec(a.shape, lambda i: (0,) * a.ndim)
    row = lambda w: pl.BlockSpec((ts, w), lambda i: (i, 0))
    memblk = pl.BlockSpec((None,) + kmem.shape[1:], lambda i: (i // tiles_per_seq, 0, 0))
    nq = wpq.shape[1]
    return pl.pallas_call(
        _mid_kernel,
        grid=(t_total // ts,),
        in_specs=[row(d), row(attn.shape[1]), row(uc.shape[1]), full(wo), full(gm), full(wmq),
                  memblk, memblk, full(wmo), full(gf), full(wpq)],
        out_specs=[row(d), row(d), row(nq)],
        out_shape=[jax.ShapeDtypeStruct((t_total, d), F32), jax.ShapeDtypeStruct((t_total, d), BF16),
                   jax.ShapeDtypeStruct((t_total, nq), BF16)],
        compiler_params=_params("arbitrary"),
        name="mid",
    )(x2, attn, uc, wo, gm, wmq, kmem, vmem, wmo, gf, wpq)


def _extract_topk(x, vals_ref):
    n = x.shape[0]
    idx = lax.broadcasted_iota(jnp.int32, x.shape, 0)
    rank = jnp.full(x.shape, TOPK, jnp.int32)
    for r in range(TOPK):
        m = jnp.max(x, axis=0, keepdims=True)
        first = jnp.min(jnp.where(x == m, idx, n), axis=0, keepdims=True)
        hit = idx == first
        rank = jnp.where(hit, r, rank)
        x = jnp.where(hit, NEG_INF, x)
        vals_ref[r:r + 1, :] = m
    return rank


def _retrieve_kernel(q_ref, keys_ref, cnt_ref, ea_ref, rb_ref, eb_ref,
                     ta_ref, tb_ref, tc_ref, cand_ref):
    tr = q_ref.shape[0]
    sa = lax.dot_general(keys_ref[0, 0], q_ref[:, :N_KEYS], NT_DIMS, preferred_element_type=F32)
    sb = lax.dot_general(keys_ref[0, 1], q_ref[:, N_KEYS:], NT_DIMS, preferred_element_type=F32)
    ra = _extract_topk(sa, ta_ref)
    rb = _extract_topk(sb, tb_ref)

    cand_ref[...] = jnp.full(cand_ref.shape, NEG_INF, F32)
    for k1 in range(TOPK):
        n, off = STAIR_N[k1], STAIR_OFF[k1]
        cand_ref[off:off + n, :] = ta_ref[k1:k1 + 1, :] + tb_ref[0:n, :]
    rc = _extract_topk(cand_ref[...], tc_ref)
    best = tc_ref[...]
    z = jnp.sum(jnp.exp(best - best[0:1, :]), axis=0, keepdims=True)

    chosen = (rc < TOPK).astype(F32)
    cnt_i = jnp.zeros((N_KEYS, tr), F32)
    for k1 in range(TOPK):
        n, off = STAIR_N[k1], STAIR_OFF[k1]
        c = jnp.sum(chosen[off:off + n, :], axis=0, keepdims=True)
        cnt_i = cnt_i + jnp.where(ra == k1, c, 0.0)
    cnt_ref[0] = cnt_i
    ea_ref[0] = jnp.where(ra < TOPK, jnp.exp(sa - ta_ref[0:1, :]), 0.0) / z
    rb_ref[0] = rb.astype(F32)
    eb_ref[0] = jnp.where(rb < TOPK, jnp.exp(sb - tb_ref[0:1, :]), 0.0)


def _retrieve(qpeer, keys, *, tr):
    t_total = qpeer.shape[0]
    out_blk = pl.BlockSpec((1, N_KEYS, tr), lambda i, h: (h, 0, i))
    return pl.pallas_call(
        _retrieve_kernel,
        grid=(t_total // tr, N_HEADS),
        in_specs=[pl.BlockSpec((tr, 2 * N_KEYS), lambda i, h: (i, h)),
                  pl.BlockSpec((1, 2, N_KEYS, N_KEYS), lambda i, h: (h, 0, 0, 0))],
        out_specs=[out_blk] * 4,
        out_shape=[jax.ShapeDtypeStruct((N_HEADS, N_KEYS, t_total), F32)] * 4,
        scratch_shapes=[pltpu.VMEM((TOPK, tr), F32), pltpu.VMEM((TOPK, tr), F32),
                        pltpu.VMEM((TOPK, tr), F32), pltpu.VMEM((STAIR_ROWS, tr), F32)],
        compiler_params=_params("arbitrary", "arbitrary"),
        name="retrieve",
    )(qpeer, keys)


def _experts_kernel(hn_ref, u_ref, v_ref, cnt_ref, ea_ref, rb_ref, eb_ref, h2_ref, g_ref,
                    o_ref, acc_ref, *, rows_per_step):
    e = pl.program_id(1)

    @pl.when(e == 0)
    def _():
        acc_ref[...] = jnp.zeros(acc_ref.shape, F32)

    s = lax.dot_general(u_ref[...], hn_ref[...], NT_DIMS, preferred_element_type=F32)
    parts = []
    for r in range(rows_per_step):
        i = e * rows_per_step + r
        gate = jnp.zeros((N_KEYS, s.shape[1]), F32)
        for h in range(N_HEADS):
            cnt = cnt_ref[h, pl.ds(i, 1), :]
            ea = ea_ref[h, pl.ds(i, 1), :]
            gate = gate + jnp.where(rb_ref[h] < cnt, eb_ref[h], 0.0) * ea
        sr = s[r * N_KEYS:(r + 1) * N_KEYS, :]
        act = 0.5 * sr * (1.0 + lax.erf(sr * (2.0 ** -0.5)))
        parts.append(gate * act)
    w = jnp.concatenate(parts, axis=0) if len(parts) > 1 else parts[0]
    acc_ref[...] += jnp.dot(w.T.astype(BF16), v_ref[...], preferred_element_type=F32)

    @pl.when(e == pl.num_programs(1) - 1)
    def _():
        o_ref[...] = _rms(h2_ref[...] + acc_ref[...], g_ref[...])


def _experts(hn3, u, v, cnt, ea, rb, eb, h2, g, *, tt, rows_per_step):
    t_total, d = hn3.shape
    n_exp = u.shape[0]
    eblk = rows_per_step * N_KEYS
    tab = pl.BlockSpec((N_HEADS, N_KEYS, tt), lambda i, e: (0, 0, i))
    return pl.pallas_call(
        functools.partial(_experts_kernel, rows_per_step=rows_per_step),
        grid=(t_total // tt, n_exp // eblk),
        in_specs=[pl.BlockSpec((tt, d), lambda i, e: (i, 0)),
                  pl.BlockSpec((eblk, d), lambda i, e: (e, 0)),
                  pl.BlockSpec((eblk, d), lambda i, e: (e, 0)),
                  tab, tab, tab, tab,
                  pl.BlockSpec((tt, d), lambda i, e: (i, 0)),
                  pl.BlockSpec(g.shape, lambda i, e: (0, 0))],
        out_specs=pl.BlockSpec((tt, d), lambda i, e: (i, 0)),
        out_shape=jax.ShapeDtypeStruct((t_total, d), F32),
        scratch_shapes=[pltpu.VMEM((tt, d), F32)],
        compiler_params=_params("arbitrary", "arbitrary"),
        name="experts",
    )(hn3, u, v, cnt, ea, rb, eb, h2, g)


def _rope_tables(seq, scale):
    inv = 1.0 / (10000.0 ** (jnp.arange(0, ROPE, 2, dtype=F32) / ROPE))
    ang = jnp.arange(seq, dtype=F32)[:, None] * inv[None, :]
    cos, sin = jnp.cos(ang), jnp.sin(ang)
    half = ROPE // 2
    ones = jnp.ones((seq, NOPE), F32)
    zeros = lambda w: jnp.zeros((seq, w), F32)
    c = jnp.concatenate([ones, cos, cos, zeros(LANES - QK_DIM)], axis=1)
    sa = jnp.concatenate([zeros(NOPE), -sin, zeros(LANES - NOPE - half)], axis=1)
    sb = jnp.concatenate([zeros(NOPE + half), sin, zeros(LANES - QK_DIM)], axis=1)
    return [c * scale, sa * scale, sb * scale]


def _pad_last(a, width):
    return jnp.pad(a, [(0, 0)] * (a.ndim - 1) + [(0, width - a.shape[-1])])


def kernel(x, mem, norm_mix_g, w_in, q_latent_g, w_q_up, kv_latent_g, w_kv_up, conv_w, conv_b,
           conv_ln_g, conv_ln_b, w_mix_out, norm_mem_g, mem_kv_norm_g, w_mem_q, w_mem_kv,
           w_mem_out, norm_ffn_g, w_peer_query, peer_sub_keys, peer_u, peer_v, final_norm_g):
    b, seq, d = x.shape
    assert norm_mix_g.shape[0] == 1, "single-layer block"
    row = lambda a: a.reshape(1, -1).astype(F32)
    x2 = x.reshape(b * seq, d)

    wi = w_in[0]
    win = jnp.concatenate([wi[:, :384], _pad_last(wi[:, 384:416], LANES), wi[:, 416:]], axis=1).astype(BF16)
    wq = _pad_last(w_q_up[0].reshape(Q_RANK, N_HEADS, QK_DIM), LANES).reshape(Q_RANK, -1).astype(BF16)
    wkv = w_kv_up[0].reshape(KV_RANK, N_HEADS, NOPE + V_DIM)
    wk = _pad_last(wkv[:, :, :NOPE], LANES).reshape(KV_RANK, -1)
    wv = wkv[:, :, NOPE:].reshape(KV_RANK, N_HEADS // 2, 2, V_DIM)
    zv = jnp.zeros_like(wv[:, :, 0])
    wv = jnp.stack([jnp.concatenate([wv[:, :, 0], zv], -1), jnp.concatenate([zv, wv[:, :, 1]], -1)], axis=2)
    wkv = jnp.concatenate([wk, wv.reshape(KV_RANK, -1)], axis=1).astype(BF16)
    tables = _rope_tables(seq, QK_DIM ** -0.5) + _rope_tables(seq, 1.0)

    qp, kp, vp, uc = _mixer_in(
        x2, row(norm_mix_g[0]), win, row(q_latent_g[0]), wq, row(kv_latent_g[0]), wkv, tables,
        conv_w[0].reshape(CONV_K, CONV_W), row(conv_b[0]), row(conv_ln_g[0]), row(conv_ln_b[0]),
        seq=seq, ts=256)
    width = N_HEADS * LANES
    attn = _attention(qp.reshape(b, seq, width), kp.reshape(b, seq, width),
                      vp.reshape(b, seq, width), tq=256, tk=256).reshape(b * seq, -1)

    kmem, vmem = _mem_kv(mem, row(mem_kv_norm_g[0]), w_mem_kv[0].astype(BF16))
    h2, hn3, qpeer = _mid(
        x2, attn, uc, w_mix_out[0].astype(BF16), row(norm_mem_g[0]), w_mem_q[0].astype(BF16),
        kmem, vmem, w_mem_out[0].astype(BF16), row(norm_ffn_g[0]), w_peer_query[0].astype(BF16),
        seq=seq, ts=256)

    cnt, ea, rb, eb = _retrieve(qpeer, peer_sub_keys[0].astype(BF16), tr=256)
    out = _experts(hn3, peer_u[0].astype(BF16), peer_v[0].astype(BF16), cnt, ea, rb, eb, h2,
                   row(final_norm_g), tt=256, rows_per_step=2)
    return out.reshape(b, seq, d)
```

```python
import functools

import jax
import jax.numpy as jnp
from jax import lax
from jax.experimental import pallas as pl
from jax.experimental.pallas import tpu as pltpu

F32 = jnp.float32
BF16 = jnp.bfloat16

EPS = 1e-6
CHUNK = 64
LANES = 128
N_HEADS = 8
NOPE = 64
ROPE = 32
QK_DIM = NOPE + ROPE
V_DIM = 64
Q_RANK = 256
KV_RANK = 128
CONV_W = 512
CONV_K = 31
CONV_HALO = 32
MEM_HEADS = 4
MEM_DIM = 256
N_KEYS = 128
TOPK = 16
STAIR_N = tuple(TOPK // (k + 1) for k in range(TOPK))
STAIR_OFF = tuple(sum(STAIR_N[:k]) for k in range(TOPK))
STAIR_ROWS = 56
NEG_INF = float("-inf")
LOG2_E = 1.4426950408889634
VMEM_LIMIT = 56 * 1024 * 1024

NT_DIMS = (((1,), (1,)), ((), ()))


def _rms(x, g):
    return x * lax.rsqrt(jnp.mean(x * x, axis=-1, keepdims=True) + EPS) * g


def _params(*sem):
    return pltpu.CompilerParams(dimension_semantics=sem, vmem_limit_bytes=VMEM_LIMIT)


def _rope(t, c, sa, sb):
    width = t.shape[1]
    reps = width // LANES
    c, sa, sb = (jnp.tile(a, (1, reps)) for a in (c, sa, sb))
    return t * c + pltpu.roll(t, width - ROPE // 2, 1) * sa + pltpu.roll(t, ROPE // 2, 1) * sb


def _mixer_in_kernel(x_ref, g_ref, win_ref, qg_ref, wq_ref, kvg_ref, wk_ref, wvt_ref,
                     qc_ref, qsa_ref, qsb_ref, kc_ref, ksa_ref, ksb_ref,
                     cw_ref, cb_ref, lng_ref, lnb_ref,
                     qp_ref, kp_ref, vt_ref, uc_ref, ubuf_ref, *, ts, tiles_per_seq):
    i = pl.program_id(0)
    xn = _rms(x_ref[...], g_ref[...])
    proj = jnp.dot(xn.astype(BF16), win_ref[...], preferred_element_type=F32)
    cq = proj[:, :Q_RANK]
    ckv = proj[:, Q_RANK:Q_RANK + KV_RANK]
    kr = proj[:, 384:512]
    q = jnp.dot(_rms(cq, qg_ref[...]).astype(BF16), wq_ref[...], preferred_element_type=F32)
    ckvn = _rms(ckv, kvg_ref[...]).astype(BF16)
    k = (jnp.dot(ckvn, wk_ref[...], preferred_element_type=F32)
         + jnp.tile(pltpu.roll(kr, NOPE, 1), (1, N_HEADS)))
    qp_ref[...] = _rope(q, qc_ref[...], qsa_ref[...], qsb_ref[...]).astype(qp_ref.dtype)
    kp_ref[...] = _rope(k, kc_ref[...], ksa_ref[...], ksb_ref[...]).astype(kp_ref.dtype)
    vt_ref[...] = lax.dot_general(wvt_ref[...], ckvn, NT_DIMS,
                                  preferred_element_type=F32).astype(vt_ref.dtype)

    u = proj[:, 512:512 + CONV_W] * jax.nn.sigmoid(proj[:, 512 + CONV_W:])

    @pl.when(i % tiles_per_seq == 0)
    def _():
        ubuf_ref[0:CONV_HALO, :] = jnp.zeros((CONV_HALO, CONV_W), F32)

    ubuf_ref[CONV_HALO:CONV_HALO + ts, :] = u
    acc = jnp.zeros((ts, CONV_W), F32) + cb_ref[...]
    first = CONV_HALO - (CONV_K - 1)
    for tap in range(CONV_K):
        acc = acc + ubuf_ref[pl.ds(first + tap, ts), :] * cw_ref[tap:tap + 1, :]
    ubuf_ref[0:CONV_HALO, :] = ubuf_ref[ts:ts + CONV_HALO, :]
    mu = jnp.mean(acc, axis=-1, keepdims=True)
    xc = acc - mu
    var = jnp.mean(xc * xc, axis=-1, keepdims=True)
    y = xc * lax.rsqrt(var + EPS) * lng_ref[...] + lnb_ref[...]
    uc_ref[...] = (y * jax.nn.sigmoid(y)).astype(uc_ref.dtype)


def _mixer_in(x2, g, win, qg, wq, kvg, wk, wvt, tables, cw, cb, lng, lnb, *, seq, ts):
    t_total, d = x2.shape
    tiles_per_seq = seq // ts
    width = N_HEADS * LANES
    vrows = N_HEADS * V_DIM
    full = lambda a: pl.BlockSpec(a.shape, lambda i: (0,) * a.ndim)
    tab = pl.BlockSpec((ts, LANES), lambda i: (i % tiles_per_seq, 0))
    row = lambda w: pl.BlockSpec((ts, w), lambda i: (i, 0))
    return pl.pallas_call(
        functools.partial(_mixer_in_kernel, ts=ts, tiles_per_seq=tiles_per_seq),
        grid=(t_total // ts,),
        in_specs=[row(d), full(g), full(win), full(qg), full(wq), full(kvg), full(wk), full(wvt)]
                 + [tab] * 6 + [full(cw), full(cb), full(lng), full(lnb)],
        out_specs=[row(width), row(width), pl.BlockSpec((vrows, ts), lambda i: (0, i)), row(CONV_W)],
        out_shape=[jax.ShapeDtypeStruct((t_total, width), BF16)] * 2
                  + [jax.ShapeDtypeStruct((vrows, t_total), BF16),
                     jax.ShapeDtypeStruct((t_total, CONV_W), BF16)],
        scratch_shapes=[pltpu.VMEM((ts + CONV_HALO, CONV_W), F32)],
        compiler_params=_params("arbitrary"),
        name="mixer_in",
    )(x2, g, win, qg, wq, kvg, wk, wvt, *tables, cw, cb, lng, lnb)


def _attn_kernel(qi_ref, ki_ref, q_ref, k_ref, vt_ref, o_ref, m_ref, l_ref, acc_ref, *, tq, tk):
    p = pl.program_id(1)
    qi = qi_ref[p]
    ki = ki_ref[p]

    @pl.when(ki == 0)
    def _():
        m_ref[...] = jnp.full(m_ref.shape, NEG_INF, F32)
        l_ref[...] = jnp.zeros(l_ref.shape, F32)
        acc_ref[...] = jnp.zeros(acc_ref.shape, F32)

    def step(masked):
        if masked:
            krow = lax.broadcasted_iota(jnp.int32, (tk, tq), 0)
            qcol = lax.broadcasted_iota(jnp.int32, (tk, tq), 1)
            allowed = (krow // CHUNK) <= (qcol // CHUNK)
        for h in range(N_HEADS):
            sl = slice(h * LANES, (h + 1) * LANES)
            vs = slice(h * V_DIM, (h + 1) * V_DIM)
            st = lax.dot_general(k_ref[:, sl], q_ref[:, sl], NT_DIMS, preferred_element_type=F32)
            if masked:
                st = jnp.where(allowed, st, -1e30)
            m_prev = m_ref[h]
            m_new = jnp.maximum(m_prev, jnp.max(st, axis=0, keepdims=True))
            alpha = jnp.exp2(m_prev - m_new)
            e = jnp.exp2(st - m_new)
            l_ref[h] = alpha * l_ref[h] + jnp.sum(e, axis=0, keepdims=True)
            acc_ref[vs, :] = alpha * acc_ref[vs, :] + jnp.dot(
                vt_ref[vs, :], e.astype(BF16), preferred_element_type=F32)
            m_ref[h] = m_new

    @pl.when(ki != qi)
    def _():
        step(False)

    @pl.when(ki == qi)
    def _():
        step(True)
        for h in range(N_HEADS):
            vs = slice(h * V_DIM, (h + 1) * V_DIM)
            acc_ref[vs, :] = acc_ref[vs, :] / l_ref[h]
        o_ref[...] = acc_ref[...].T.astype(o_ref.dtype)


def _attention(qp, kp, vt, *, tq, tk):
    b, s, width = qp.shape
    vrows = vt.shape[0]
    assert tq == tk and tq % CHUNK == 0 and s % tq == 0
    nq = s // tq
    pairs = [(qi, ki) for qi in range(nq) for ki in range(qi + 1)]
    qi_arr = jnp.asarray([p[0] for p in pairs], jnp.int32)
    ki_arr = jnp.asarray([p[1] for p in pairs], jnp.int32)
    grid_spec = pltpu.PrefetchScalarGridSpec(
        num_scalar_prefetch=2,
        grid=(b, len(pairs)),
        in_specs=[
            pl.BlockSpec((None, tq, width), lambda bi, p, qi, ki: (bi, qi[p], 0)),
            pl.BlockSpec((None, tk, width), lambda bi, p, qi, ki: (bi, ki[p], 0)),
            pl.BlockSpec((vrows, tk), lambda bi, p, qi, ki: (0, bi * nq + ki[p])),
        ],
        out_specs=pl.BlockSpec((None, tq, vrows), lambda bi, p, qi, ki: (bi, qi[p], 0)),
        scratch_shapes=[pltpu.VMEM((N_HEADS, 1, tq), F32), pltpu.VMEM((N_HEADS, 1, tq), F32),
                        pltpu.VMEM((vrows, tq), F32)],
    )
    return pl.pallas_call(
        functools.partial(_attn_kernel, tq=tq, tk=tk),
        grid_spec=grid_spec,
        out_shape=jax.ShapeDtypeStruct((b, s, vrows), BF16),
        compiler_params=_params("arbitrary", "arbitrary"),
        name="attention",
    )(qi_arr, ki_arr, qp, kp, vt)


def _mem_kv_kernel(mem_ref, g_ref, w_ref, k_ref, v_ref):
    d = mem_ref.shape[-1]
    kv = jnp.dot(_rms(mem_ref[...], g_ref[...]).astype(BF16), w_ref[...],
                 preferred_element_type=F32)
    k_ref[...] = kv[:, :d].astype(k_ref.dtype)
    v_ref[...] = kv[:, d:].astype(v_ref.dtype)


def _mem_kv(mem, g, w):
    b, m, d = mem.shape
    blk = pl.BlockSpec((None, m, d), lambda i: (i, 0, 0))
    return pl.pallas_call(
        _mem_kv_kernel,
        grid=(b,),
        in_specs=[blk, pl.BlockSpec(g.shape, lambda i: (0, 0)), pl.BlockSpec(w.shape, lambda i: (0, 0))],
        out_specs=[blk, blk],
        out_shape=[jax.ShapeDtypeStruct((b, m, d), BF16)] * 2,
        compiler_params=_params("arbitrary"),
        name="mem_kv",
    )(mem, g, w)


def _mid_kernel(x_ref, attn_ref, uc_ref, wo_ref, gm_ref, wmq_ref, km_ref, vm_ref, wmo_ref,
                gf_ref, wpq_ref, h2_ref, hn_ref, qp_ref):
    half = attn_ref.shape[1]
    h1 = (x_ref[...]
          + jnp.dot(attn_ref[...], wo_ref[0:half, :], preferred_element_type=F32)
          + jnp.dot(uc_ref[...], wo_ref[half:, :], preferred_element_type=F32))
    hn = _rms(h1, gm_ref[...]).astype(BF16)
    q = (jnp.dot(hn, wmq_ref[...], preferred_element_type=F32) * (MEM_DIM ** -0.5)).astype(BF16)
    outs = []
    for h in range(MEM_HEADS):
        sl = slice(h * MEM_DIM, (h + 1) * MEM_DIM)
        s = lax.dot_general(q[:, sl], km_ref[:, sl], NT_DIMS, preferred_element_type=F32)
        e = jnp.exp(s - jnp.max(s, axis=1, keepdims=True))
        o = jnp.dot(e.astype(BF16), vm_ref[:, sl], preferred_element_type=F32)
        outs.append(o / jnp.sum(e, axis=1, keepdims=True))
    o = jnp.concatenate(outs, axis=1).astype(BF16)
    h2 = h1 + jnp.dot(o, wmo_ref[...], preferred_element_type=F32)
    h2_ref[...] = h2
    hn3 = _rms(h2, gf_ref[...]).astype(BF16)
    hn_ref[...] = hn3
    qp_ref[...] = jnp.dot(hn3, wpq_ref[...], preferred_element_type=F32).astype(qp_ref.dtype)


def _mid(x2, attn, uc, wo, gm, wmq, kmem, vmem, wmo, gf, wpq, *, seq, ts):
    t_total, d = x2.shape
    tiles_per_seq = seq // ts
    full = lambda a: pl.BlockSpec(a.shape, lambda i: (0,) * a.ndim)
    row = lambda w: pl.BlockSpec((ts, w), lambda i: (i, 0))
    memblk = pl.BlockSpec((None,) + kmem.shape[1:], lambda i: (i // tiles_per_seq, 0, 0))
    nq = wpq.shape[1]
    return pl.pallas_call(
        _mid_kernel,
        grid=(t_total // ts,),
        in_specs=[row(d), row(attn.shape[1]), row(uc.shape[1]), full(wo), full(gm), full(wmq),
                  memblk, memblk, full(wmo), full(gf), full(wpq)],
        out_specs=[row(d), row(d), row(nq)],
        out_shape=[jax.ShapeDtypeStruct((t_total, d), F32), jax.ShapeDtypeStruct((t_total, d), BF16),
                   jax.ShapeDtypeStruct((t_total, nq), BF16)],
        compiler_params=_params("arbitrary"),
        name="mid",
    )(x2, attn, uc, wo, gm, wmq, kmem, vmem, wmo, gf, wpq)


def _extract_topk(x, vals_ref):
    n = x.shape[0]
    idx = lax.broadcasted_iota(jnp.int32, x.shape, 0)
    rank = jnp.full(x.shape, TOPK, jnp.int32)
    for r in range(TOPK):
        m = jnp.max(x, axis=0, keepdims=True)
        first = jnp.min(jnp.where(x == m, idx, n), axis=0, keepdims=True)
        hit = idx == first
        rank = jnp.where(hit, r, rank)
        x = jnp.where(hit, NEG_INF, x)
        vals_ref[r:r + 1, :] = m
    return rank


def _retrieve_kernel(q_ref, keys_ref, cnt_ref, ea_ref, rb_ref, eb_ref,
                     ta_ref, tb_ref, tc_ref, cand_ref):
    tr = q_ref.shape[0]
    sa = lax.dot_general(keys_ref[0, 0], q_ref[:, :N_KEYS], NT_DIMS, preferred_element_type=F32)
    sb = lax.dot_general(keys_ref[0, 1], q_ref[:, N_KEYS:], NT_DIMS, preferred_element_type=F32)
    ra = _extract_topk(sa, ta_ref)
    rb = _extract_topk(sb, tb_ref)

    cand_ref[...] = jnp.full(cand_ref.shape, NEG_INF, F32)
    for k1 in range(TOPK):
        n, off = STAIR_N[k1], STAIR_OFF[k1]
        cand_ref[off:off + n, :] = ta_ref[k1:k1 + 1, :] + tb_ref[0:n, :]
    rc = _extract_topk(cand_ref[...], tc_ref)
    best = tc_ref[...]
    z = jnp.sum(jnp.exp(best - best[0:1, :]), axis=0, keepdims=True)

    chosen = (rc < TOPK).astype(F32)
    cnt_i = jnp.zeros((N_KEYS, tr), F32)
    for k1 in range(TOPK):
        n, off = STAIR_N[k1], STAIR_OFF[k1]
        c = jnp.sum(chosen[off:off + n, :], axis=0, keepdims=True)
        cnt_i = cnt_i + jnp.where(ra == k1, c, 0.0)
    cnt_ref[0] = cnt_i
    ea_ref[0] = jnp.where(ra < TOPK, jnp.exp(sa - ta_ref[0:1, :]), 0.0) * (0.5 / z)
    rb_ref[0] = rb.astype(F32).astype(rb_ref.dtype)
    eb_ref[0] = jnp.where(rb < TOPK, jnp.exp(sb - tb_ref[0:1, :]), 0.0).astype(eb_ref.dtype)


def _retrieve(qpeer, keys, *, tr):
    t_total = qpeer.shape[0]
    out_blk = pl.BlockSpec((1, N_KEYS, tr), lambda i, h: (h, 0, i))
    return pl.pallas_call(
        _retrieve_kernel,
        grid=(t_total // tr, N_HEADS),
        in_specs=[pl.BlockSpec((tr, 2 * N_KEYS), lambda i, h: (i, h)),
                  pl.BlockSpec((1, 2, N_KEYS, N_KEYS), lambda i, h: (h, 0, 0, 0))],
        out_specs=[out_blk] * 4,
        out_shape=[jax.ShapeDtypeStruct((N_HEADS, N_KEYS, t_total), dt) for dt in (F32, F32, BF16, BF16)],
        scratch_shapes=[pltpu.VMEM((TOPK, tr), F32), pltpu.VMEM((TOPK, tr), F32),
                        pltpu.VMEM((TOPK, tr), F32), pltpu.VMEM((STAIR_ROWS, tr), F32)],
        compiler_params=_params("arbitrary", "arbitrary"),
        name="retrieve",
    )(qpeer, keys)


def _experts_kernel(hn_ref, u_ref, vt_ref, cnt_ref, ea_ref, rb_ref, eb_ref, h2_ref, g_ref,
                    o_ref, acc_ref, hnt_ref, *, rows_per_step):
    e = pl.program_id(1)

    @pl.when(e == 0)
    def _():
        acc_ref[...] = jnp.zeros(acc_ref.shape, F32)
        hnt_ref[...] = hn_ref[...].astype(F32).T.astype(BF16)

    s = jnp.dot(u_ref[...], hnt_ref[...], preferred_element_type=F32)
    zero = jnp.zeros((), BF16)
    parts = []
    for r in range(rows_per_step):
        i = e * rows_per_step + r
        gate = None
        for h in range(N_HEADS):
            cnt = cnt_ref[h, pl.ds(i, 1), :].astype(BF16)
            ea = ea_ref[h, pl.ds(i, 1), :].astype(BF16)
            term = jnp.where(rb_ref[h] < cnt, eb_ref[h], zero) * ea
            gate = term if gate is None else gate + term
        sr = s[r * N_KEYS:(r + 1) * N_KEYS, :]
        act = sr * (1.0 + lax.erf(sr * (2.0 ** -0.5)))
        parts.append(gate * act.astype(BF16))
    w = jnp.concatenate(parts, axis=0) if len(parts) > 1 else parts[0]
    acc_ref[...] += jnp.dot(vt_ref[...], w, preferred_element_type=F32)

    @pl.when(e == pl.num_programs(1) - 1)
    def _():
        o_ref[...] = _rms(h2_ref[...] + acc_ref[...].T, g_ref[...])


def _experts(hn3, u, vt, cnt, ea, rb, eb, h2, g, *, tt, rows_per_step):
    t_total, d = hn3.shape
    n_exp = u.shape[0]
    eblk = rows_per_step * N_KEYS
    tab = pl.BlockSpec((N_HEADS, N_KEYS, tt), lambda i, e: (0, 0, i))
    return pl.pallas_call(
        functools.partial(_experts_kernel, rows_per_step=rows_per_step),
        grid=(t_total // tt, n_exp // eblk),
        in_specs=[pl.BlockSpec((tt, d), lambda i, e: (i, 0)),
                  pl.BlockSpec((eblk, d), lambda i, e: (e, 0)),
                  pl.BlockSpec((d, eblk), lambda i, e: (0, e)),
                  tab, tab, tab, tab,
                  pl.BlockSpec((tt, d), lambda i, e: (i, 0)),
                  pl.BlockSpec(g.shape, lambda i, e: (0, 0))],
        out_specs=pl.BlockSpec((tt, d), lambda i, e: (i, 0)),
        out_shape=jax.ShapeDtypeStruct((t_total, d), F32),
        scratch_shapes=[pltpu.VMEM((d, tt), F32), pltpu.VMEM((d, tt), BF16)],
        compiler_params=_params("arbitrary", "arbitrary"),
        name="experts",
    )(hn3, u, vt, cnt, ea, rb, eb, h2, g)


def _rope_tables(seq, scale):
    inv = 1.0 / (10000.0 ** (jnp.arange(0, ROPE, 2, dtype=F32) / ROPE))
    ang = jnp.arange(seq, dtype=F32)[:, None] * inv[None, :]
    cos, sin = jnp.cos(ang), jnp.sin(ang)
    half = ROPE // 2
    ones = jnp.ones((seq, NOPE), F32)
    zeros = lambda w: jnp.zeros((seq, w), F32)
    c = jnp.concatenate([ones, cos, cos, zeros(LANES - QK_DIM)], axis=1)
    sa = jnp.concatenate([zeros(NOPE), -sin, zeros(LANES - NOPE - half)], axis=1)
    sb = jnp.concatenate([zeros(NOPE + half), sin, zeros(LANES - QK_DIM)], axis=1)
    return [c * scale, sa * scale, sb * scale]


def _pad_last(a, width):
    return jnp.pad(a, [(0, 0)] * (a.ndim - 1) + [(0, width - a.shape[-1])])


def kernel(x, mem, norm_mix_g, w_in, q_latent_g, w_q_up, kv_latent_g, w_kv_up, conv_w, conv_b,
           conv_ln_g, conv_ln_b, w_mix_out, norm_mem_g, mem_kv_norm_g, w_mem_q, w_mem_kv,
           w_mem_out, norm_ffn_g, w_peer_query, peer_sub_keys, peer_u, peer_v, final_norm_g):
    b, seq, d = x.shape
    assert norm_mix_g.shape[0] == 1, "single-layer block"
    row = lambda a: a.reshape(1, -1).astype(F32)
    x2 = x.reshape(b * seq, d)

    wi = w_in[0]
    win = jnp.concatenate([wi[:, :384], _pad_last(wi[:, 384:416], LANES), wi[:, 416:]], axis=1).astype(BF16)
    wq = _pad_last(w_q_up[0].reshape(Q_RANK, N_HEADS, QK_DIM), LANES).reshape(Q_RANK, -1).astype(BF16)
    wkv = w_kv_up[0].reshape(KV_RANK, N_HEADS, NOPE + V_DIM)
    wk = _pad_last(wkv[:, :, :NOPE], LANES).reshape(KV_RANK, -1).astype(BF16)
    wvt = wkv[:, :, NOPE:].reshape(KV_RANK, -1).T.astype(BF16)
    tables = _rope_tables(seq, QK_DIM ** -0.5 * LOG2_E) + _rope_tables(seq, 1.0)

    qp, kp, vt, uc = _mixer_in(
        x2, row(norm_mix_g[0]), win, row(q_latent_g[0]), wq, row(kv_latent_g[0]), wk, wvt, tables,
        conv_w[0].reshape(CONV_K, CONV_W), row(conv_b[0]), row(conv_ln_g[0]), row(conv_ln_b[0]),
        seq=seq, ts=256)
    width = N_HEADS * LANES
    attn = _attention(qp.reshape(b, seq, width), kp.reshape(b, seq, width), vt,
                      tq=256, tk=256).reshape(b * seq, -1)

    kmem, vmem = _mem_kv(mem, row(mem_kv_norm_g[0]), w_mem_kv[0].astype(BF16))
    h2, hn3, qpeer = _mid(
        x2, attn, uc, w_mix_out[0].astype(BF16), row(norm_mem_g[0]), w_mem_q[0].astype(BF16),
        kmem, vmem, w_mem_out[0].astype(BF16), row(norm_ffn_g[0]), w_peer_query[0].astype(BF16),
        seq=seq, ts=256)

    cnt, ea, rb, eb = _retrieve(qpeer, peer_sub_keys[0].astype(BF16), tr=256)
    out = _experts(hn3, peer_u[0].astype(BF16), peer_v[0].T.astype(BF16), cnt, ea, rb, eb, h2,
                   row(final_norm_g), tt=512, rows_per_step=8)
    return out.reshape(b, seq, d)
```

```python
import functools

import jax
import jax.numpy as jnp
from jax import lax
from jax.experimental import pallas as pl
from jax.experimental.pallas import tpu as pltpu

F32 = jnp.float32
BF16 = jnp.bfloat16

EPS = 1e-6
CHUNK = 64
LANES = 128
N_HEADS = 8
NOPE = 64
ROPE = 32
QK_DIM = NOPE + ROPE
V_DIM = 64
Q_RANK = 256
KV_RANK = 128
CONV_W = 512
CONV_K = 31
CONV_HALO = 32
MEM_HEADS = 4
MEM_DIM = 256
N_KEYS = 128
TOPK = 16
STAIR_N = tuple(TOPK // (k + 1) for k in range(TOPK))
STAIR_OFF = tuple(sum(STAIR_N[:k]) for k in range(TOPK))
STAIR_ROWS = 56
NEG_INF = float("-inf")
LOG2_E = 1.4426950408889634
VMEM_LIMIT = 56 * 1024 * 1024

NT_DIMS = (((1,), (1,)), ((), ()))


def _rms(x, g):
    return x * lax.rsqrt(jnp.mean(x * x, axis=-1, keepdims=True) + EPS) * g


def _params(*sem):
    return pltpu.CompilerParams(dimension_semantics=sem, vmem_limit_bytes=VMEM_LIMIT)


def _rope(t, c, sa, sb):
    width = t.shape[1]
    reps = width // LANES
    c, sa, sb = (jnp.tile(a, (1, reps)) for a in (c, sa, sb))
    return t * c + pltpu.roll(t, width - ROPE // 2, 1) * sa + pltpu.roll(t, ROPE // 2, 1) * sb


def _mixer_in_kernel(x_ref, g_ref, win_ref, qg_ref, wq_ref, kvg_ref, wk_ref, wvt_ref,
                     qc_ref, qsa_ref, qsb_ref, kc_ref, ksa_ref, ksb_ref,
                     cw_ref, cb_ref, lng_ref, lnb_ref,
                     qp_ref, kp_ref, vt_ref, uc_ref, ubuf_ref, *, ts, tiles_per_seq):
    i = pl.program_id(0)
    xn = _rms(x_ref[...], g_ref[...])
    proj = jnp.dot(xn.astype(BF16), win_ref[...], preferred_element_type=F32)
    cq = proj[:, :Q_RANK]
    ckv = proj[:, Q_RANK:Q_RANK + KV_RANK]
    kr = proj[:, 384:512]
    q = jnp.dot(_rms(cq, qg_ref[...]).astype(BF16), wq_ref[...], preferred_element_type=F32)
    ckvn = _rms(ckv, kvg_ref[...]).astype(BF16)
    k = (jnp.dot(ckvn, wk_ref[...], preferred_element_type=F32)
         + jnp.tile(pltpu.roll(kr, NOPE, 1), (1, N_HEADS)))
    qp_ref[...] = _rope(q, qc_ref[...], qsa_ref[...], qsb_ref[...]).astype(qp_ref.dtype)
    kp_ref[...] = _rope(k, kc_ref[...], ksa_ref[...], ksb_ref[...]).astype(kp_ref.dtype)
    vt_ref[...] = lax.dot_general(wvt_ref[...], ckvn, NT_DIMS,
                                  preferred_element_type=F32).astype(vt_ref.dtype)

    u = proj[:, 512:512 + CONV_W] * jax.nn.sigmoid(proj[:, 512 + CONV_W:])

    @pl.when(i % tiles_per_seq == 0)
    def _():
        ubuf_ref[0:CONV_HALO, :] = jnp.zeros((CONV_HALO, CONV_W), F32)

    ubuf_ref[CONV_HALO:CONV_HALO + ts, :] = u
    acc = jnp.zeros((ts, CONV_W), F32) + cb_ref[...]
    first = CONV_HALO - (CONV_K - 1)
    for tap in range(CONV_K):
        acc = acc + ubuf_ref[pl.ds(first + tap, ts), :] * cw_ref[tap:tap + 1, :]
    ubuf_ref[0:CONV_HALO, :] = ubuf_ref[ts:ts + CONV_HALO, :]
    mu = jnp.mean(acc, axis=-1, keepdims=True)
    xc = acc - mu
    var = jnp.mean(xc * xc, axis=-1, keepdims=True)
    y = xc * lax.rsqrt(var + EPS) * lng_ref[...] + lnb_ref[...]
    uc_ref[...] = (y * jax.nn.sigmoid(y)).astype(uc_ref.dtype)


def _mixer_in(x2, g, win, qg, wq, kvg, wk, wvt, tables, cw, cb, lng, lnb, *, seq, ts):
    t_total, d = x2.shape
    tiles_per_seq = seq // ts
    width = N_HEADS * LANES
    vrows = N_HEADS * V_DIM
    full = lambda a: pl.BlockSpec(a.shape, lambda i: (0,) * a.ndim)
    tab = pl.BlockSpec((ts, LANES), lambda i: (i % tiles_per_seq, 0))
    row = lambda w: pl.BlockSpec((ts, w), lambda i: (i, 0))
    return pl.pallas_call(
        functools.partial(_mixer_in_kernel, ts=ts, tiles_per_seq=tiles_per_seq),
        grid=(t_total // ts,),
        in_specs=[row(d), full(g), full(win), full(qg), full(wq), full(kvg), full(wk), full(wvt)]
                 + [tab] * 6 + [full(cw), full(cb), full(lng), full(lnb)],
        out_specs=[row(width), row(width), pl.BlockSpec((vrows, ts), lambda i: (0, i)), row(CONV_W)],
        out_shape=[jax.ShapeDtypeStruct((t_total, width), BF16)] * 2
                  + [jax.ShapeDtypeStruct((vrows, t_total), BF16),
                     jax.ShapeDtypeStruct((t_total, CONV_W), BF16)],
        scratch_shapes=[pltpu.VMEM((ts + CONV_HALO, CONV_W), F32)],
        compiler_params=_params("arbitrary"),
        name="mixer_in",
    )(x2, g, win, qg, wq, kvg, wk, wvt, *tables, cw, cb, lng, lnb)


def _attn_kernel(qi_ref, ki_ref, q_ref, k_ref, vt_ref, o_ref, m_ref, l_ref, acc_ref,
                 s_ref, p_ref, *, tq, tk):
    p = pl.program_id(1)
    qi = qi_ref[p]
    ki = ki_ref[p]

    @pl.when(ki == 0)
    def _():
        m_ref[...] = jnp.full(m_ref.shape, NEG_INF, F32)
        l_ref[...] = jnp.zeros(l_ref.shape, F32)
        acc_ref[...] = jnp.zeros(acc_ref.shape, F32)

    def step(masked):
        if masked:
            krow = lax.broadcasted_iota(jnp.int32, (tk, tq), 0)
            qcol = lax.broadcasted_iota(jnp.int32, (tk, tq), 1)
            allowed = (krow // CHUNK) <= (qcol // CHUNK)
        m_new, alpha = [], []
        for h in range(N_HEADS):
            sl = slice(h * LANES, (h + 1) * LANES)
            st = lax.dot_general(k_ref[:, sl], q_ref[:, sl], NT_DIMS, preferred_element_type=F32)
            if masked:
                st = jnp.where(allowed, st, -1e30)
            s_ref[h] = st
            m_prev = m_ref[h]
            m_new.append(jnp.maximum(m_prev, jnp.max(st, axis=0, keepdims=True)))
            alpha.append(jnp.exp2(m_prev - m_new[h]))
            m_ref[h] = m_new[h]
        for h in range(N_HEADS):
            e = jnp.exp2(s_ref[h] - m_new[h])
            l_ref[h] = alpha[h] * l_ref[h] + jnp.sum(e, axis=0, keepdims=True)
            p_ref[h] = e.astype(BF16)
        for h in range(N_HEADS):
            vs = slice(h * V_DIM, (h + 1) * V_DIM)
            acc_ref[vs, :] = alpha[h] * acc_ref[vs, :] + jnp.dot(
                vt_ref[vs, :], p_ref[h], preferred_element_type=F32)

    @pl.when(ki != qi)
    def _():
        step(False)

    @pl.when(ki == qi)
    def _():
        step(True)
        for h in range(N_HEADS):
            vs = slice(h * V_DIM, (h + 1) * V_DIM)
            acc_ref[vs, :] = acc_ref[vs, :] / l_ref[h]
        o_ref[...] = acc_ref[...].T.astype(o_ref.dtype)


def _attention(qp, kp, vt, *, tq, tk):
    b, s, width = qp.shape
    vrows = vt.shape[0]
    assert tq == tk and tq % CHUNK == 0 and s % tq == 0
    nq = s // tq
    pairs = [(qi, ki) for qi in range(nq) for ki in range(qi + 1)]
    qi_arr = jnp.asarray([p[0] for p in pairs], jnp.int32)
    ki_arr = jnp.asarray([p[1] for p in pairs], jnp.int32)
    grid_spec = pltpu.PrefetchScalarGridSpec(
        num_scalar_prefetch=2,
        grid=(b, len(pairs)),
        in_specs=[
            pl.BlockSpec((None, tq, width), lambda bi, p, qi, ki: (bi, qi[p], 0)),
            pl.BlockSpec((None, tk, width), lambda bi, p, qi, ki: (bi, ki[p], 0)),
            pl.BlockSpec((vrows, tk), lambda bi, p, qi, ki: (0, bi * nq + ki[p])),
        ],
        out_specs=pl.BlockSpec((None, tq, vrows), lambda bi, p, qi, ki: (bi, qi[p], 0)),
        scratch_shapes=[pltpu.VMEM((N_HEADS, 1, tq), F32), pltpu.VMEM((N_HEADS, 1, tq), F32),
                        pltpu.VMEM((vrows, tq), F32),
                        pltpu.VMEM((N_HEADS, tk, tq), F32), pltpu.VMEM((N_HEADS, tk, tq), BF16)],
    )
    return pl.pallas_call(
        functools.partial(_attn_kernel, tq=tq, tk=tk),
        grid_spec=grid_spec,
        out_shape=jax.ShapeDtypeStruct((b, s, vrows), BF16),
        compiler_params=_params("arbitrary", "arbitrary"),
        name="attention",
    )(qi_arr, ki_arr, qp, kp, vt)


def _mem_kv_kernel(mem_ref, g_ref, w_ref, k_ref, v_ref):
    d = mem_ref.shape[-1]
    kv = jnp.dot(_rms(mem_ref[...], g_ref[...]).astype(BF16), w_ref[...],
                 preferred_element_type=F32)
    k_ref[...] = kv[:, :d].astype(k_ref.dtype)
    v_ref[...] = kv[:, d:].astype(v_ref.dtype)


def _mem_kv(mem, g, w):
    b, m, d = mem.shape
    blk = pl.BlockSpec((None, m, d), lambda i: (i, 0, 0))
    return pl.pallas_call(
        _mem_kv_kernel,
        grid=(b,),
        in_specs=[blk, pl.BlockSpec(g.shape, lambda i: (0, 0)), pl.BlockSpec(w.shape, lambda i: (0, 0))],
        out_specs=[blk, blk],
        out_shape=[jax.ShapeDtypeStruct((b, m, d), BF16)] * 2,
        compiler_params=_params("arbitrary"),
        name="mem_kv",
    )(mem, g, w)


def _mid_kernel(x_ref, attn_ref, uc_ref, wo_ref, gm_ref, wmq_ref, km_ref, vm_ref, wmo_ref,
                gf_ref, wpq_ref, h2_ref, hn_ref, qp_ref):
    half = attn_ref.shape[1]
    h1 = (x_ref[...]
          + jnp.dot(attn_ref[...], wo_ref[0:half, :], preferred_element_type=F32)
          + jnp.dot(uc_ref[...], wo_ref[half:, :], preferred_element_type=F32))
    hn = _rms(h1, gm_ref[...]).astype(BF16)
    q = (jnp.dot(hn, wmq_ref[...], preferred_element_type=F32) * (MEM_DIM ** -0.5)).astype(BF16)
    outs = []
    for h in range(MEM_HEADS):
        sl = slice(h * MEM_DIM, (h + 1) * MEM_DIM)
        s = lax.dot_general(q[:, sl], km_ref[:, sl], NT_DIMS, preferred_element_type=F32)
        e = jnp.exp(s - jnp.max(s, axis=1, keepdims=True))
        o = jnp.dot(e.astype(BF16), vm_ref[:, sl], preferred_element_type=F32)
        outs.append(o / jnp.sum(e, axis=1, keepdims=True))
    o = jnp.concatenate(outs, axis=1).astype(BF16)
    h2 = h1 + jnp.dot(o, wmo_ref[...], preferred_element_type=F32)
    h2_ref[...] = h2
    hn3 = _rms(h2, gf_ref[...]).astype(BF16)
    hn_ref[...] = hn3
    qp_ref[...] = jnp.dot(hn3, wpq_ref[...], preferred_element_type=F32).astype(qp_ref.dtype)


def _mid(x2, attn, uc, wo, gm, wmq, kmem, vmem, wmo, gf, wpq, *, seq, ts):
    t_total, d = x2.shape
    tiles_per_seq = seq // ts
    full = lambda a: pl.BlockSpec(a.shape, lambda i: (0,) * a.ndim)
    row = lambda w: pl.BlockSpec((ts, w), lambda i: (i, 0))
    memblk = pl.BlockSpec((None,) + kmem.shape[1:], lambda i: (i // tiles_per_seq, 0, 0))
    nq = wpq.shape[1]
    return pl.pallas_call(
        _mid_kernel,
        grid=(t_total // ts,),
        in_specs=[row(d), row(attn.shape[1]), row(uc.shape[1]), full(wo), full(gm), full(wmq),
                  memblk, memblk, full(wmo), full(gf), full(wpq)],
        out_specs=[row(d), row(d), row(nq)],
        out_shape=[jax.ShapeDtypeStruct((t_total, d), F32), jax.ShapeDtypeStruct((t_total, d), BF16),
                   jax.ShapeDtypeStruct((t_total, nq), BF16)],
        compiler_params=_params("arbitrary"),
        name="mid",
    )(x2, attn, uc, wo, gm, wmq, kmem, vmem, wmo, gf, wpq)


def _extract_topk(x, vals_ref):
    n = x.shape[0]
    idx = lax.broadcasted_iota(jnp.int32, x.shape, 0)
    rank = jnp.full(x.shape, TOPK, jnp.int32)
    for r in range(TOPK):
        m = jnp.max(x, axis=0, keepdims=True)
        first = jnp.min(jnp.where(x == m, idx, n), axis=0, keepdims=True)
        hit = idx == first
        rank = jnp.where(hit, r, rank)
        x = jnp.where(hit, NEG_INF, x)
        vals_ref[r:r + 1, :] = m
    return rank


def _retrieve_kernel(q_ref, keys_ref, cnt_ref, ea_ref, rb_ref, eb_ref,
                     ta_ref, tb_ref, tc_ref, cand_ref):
    tr = q_ref.shape[0]
    sa = lax.dot_general(keys_ref[0, 0], q_ref[:, :N_KEYS], NT_DIMS, preferred_element_type=F32)
    sb = lax.dot_general(keys_ref[0, 1], q_ref[:, N_KEYS:], NT_DIMS, preferred_element_type=F32)
    ra = _extract_topk(sa, ta_ref)
    rb = _extract_topk(sb, tb_ref)

    cand_ref[...] = jnp.full(cand_ref.shape, NEG_INF, F32)
    for k1 in range(TOPK):
        n, off = STAIR_N[k1], STAIR_OFF[k1]
        cand_ref[off:off + n, :] = ta_ref[k1:k1 + 1, :] + tb_ref[0:n, :]
    rc = _extract_topk(cand_ref[...], tc_ref)
    best = tc_ref[...]
    z = jnp.sum(jnp.exp(best - best[0:1, :]), axis=0, keepdims=True)

    chosen = (rc < TOPK).astype(F32)
    cnt_i = jnp.zeros((N_KEYS, tr), F32)
    for k1 in range(TOPK):
        n, off = STAIR_N[k1], STAIR_OFF[k1]
        c = jnp.sum(chosen[off:off + n, :], axis=0, keepdims=True)
        cnt_i = cnt_i + jnp.where(ra == k1, c, 0.0)
    cnt_ref[0] = cnt_i
    ea_ref[0] = jnp.where(ra < TOPK, jnp.exp(sa - ta_ref[0:1, :]), 0.0) * (0.5 / z)
    rb_ref[0] = rb.astype(F32).astype(rb_ref.dtype)
    eb_ref[0] = jnp.where(rb < TOPK, jnp.exp(sb - tb_ref[0:1, :]), 0.0).astype(eb_ref.dtype)


def _retrieve(qpeer, keys, *, tr):
    t_total = qpeer.shape[0]
    out_blk = pl.BlockSpec((1, N_KEYS, tr), lambda i, h: (h, 0, i))
    return pl.pallas_call(
        _retrieve_kernel,
        grid=(t_total // tr, N_HEADS),
        in_specs=[pl.BlockSpec((tr, 2 * N_KEYS), lambda i, h: (i, h)),
                  pl.BlockSpec((1, 2, N_KEYS, N_KEYS), lambda i, h: (h, 0, 0, 0))],
        out_specs=[out_blk] * 4,
        out_shape=[jax.ShapeDtypeStruct((N_HEADS, N_KEYS, t_total), dt) for dt in (F32, F32, BF16, BF16)],
        scratch_shapes=[pltpu.VMEM((TOPK, tr), F32), pltpu.VMEM((TOPK, tr), F32),
                        pltpu.VMEM((TOPK, tr), F32), pltpu.VMEM((STAIR_ROWS, tr), F32)],
        compiler_params=_params("arbitrary", "arbitrary"),
        name="retrieve",
    )(qpeer, keys)


def _experts_kernel(hn_ref, u_ref, vt_ref, cnt_ref, ea_ref, rb_ref, eb_ref, h2_ref, g_ref,
                    o_ref, acc_ref, hnt_ref, s0_ref, s1_ref, *, rows_per_step, rows_per_chunk,
                    n_blocks, n_work):
    n = pl.program_id(0)
    new_tile = n % n_blocks == 0

    @pl.when(n == 0)
    def _():
        acc_ref[...] = jnp.zeros(acc_ref.shape, F32)
        s1_ref[...] = jnp.zeros(s1_ref.shape, F32)

    @pl.when(new_tile)
    def _():
        hnt_ref[...] = hn_ref[...].astype(F32).T.astype(BF16)

    def body(s_write, s_read):
        s_write[...] = jnp.dot(u_ref[...], hnt_ref[...], preferred_element_type=F32)
        zero = jnp.zeros((), BF16)
        for c in range(rows_per_step // rows_per_chunk):
            cols = slice(c * rows_per_chunk * N_KEYS, (c + 1) * rows_per_chunk * N_KEYS)
            parts = []
            for r in range(c * rows_per_chunk, (c + 1) * rows_per_chunk):
                gate = None
                for h in range(N_HEADS):
                    cnt = cnt_ref[h, r:r + 1, :].astype(BF16)
                    ea = ea_ref[h, r:r + 1, :].astype(BF16)
                    term = jnp.where(rb_ref[h] < cnt, eb_ref[h], zero) * ea
                    gate = term if gate is None else gate + term
                sr = s_read[r * N_KEYS:(r + 1) * N_KEYS, :]
                act = sr * (1.0 + lax.erf(sr * (2.0 ** -0.5)))
                parts.append(gate * act.astype(BF16))
            w = jnp.concatenate(parts, axis=0)
            acc_ref[...] += jnp.dot(vt_ref[:, cols], w, preferred_element_type=F32)

    @pl.when(n % 2 == 0)
    def _():
        body(s0_ref, s1_ref)

    @pl.when(n % 2 == 1)
    def _():
        body(s1_ref, s0_ref)

    @pl.when(new_tile & (n > 0))
    def _():
        o_ref[...] = _rms(h2_ref[...] + acc_ref[...].T, g_ref[...])
        acc_ref[...] = jnp.zeros(acc_ref.shape, F32)


def _experts(hn3, u, vt, cnt, ea, rb, eb, h2, g, *, tt, rows_per_step, rows_per_chunk):
    t_total, d = hn3.shape
    n_exp = u.shape[0]
    eblk = rows_per_step * N_KEYS
    n_blocks = n_exp // eblk
    n_work = (t_total // tt) * n_blocks
    cur = lambda n: jnp.minimum(n, n_work - 1)
    prev = lambda n: jnp.maximum(n - 1, 0)
    tab = pl.BlockSpec((N_HEADS, N_KEYS, tt), lambda n: (0, 0, prev(n) // n_blocks))
    rowtab = pl.BlockSpec((N_HEADS, rows_per_step, tt),
                          lambda n: (0, prev(n) % n_blocks, prev(n) // n_blocks))
    return pl.pallas_call(
        functools.partial(_experts_kernel, rows_per_step=rows_per_step,
                          rows_per_chunk=rows_per_chunk, n_blocks=n_blocks, n_work=n_work),
        grid=(n_work + 1,),
        in_specs=[pl.BlockSpec((tt, d), lambda n: (cur(n) // n_blocks, 0)),
                  pl.BlockSpec((eblk, d), lambda n: (cur(n) % n_blocks, 0)),
                  pl.BlockSpec((d, eblk), lambda n: (0, prev(n) % n_blocks)),
                  rowtab, rowtab, tab, tab,
                  pl.BlockSpec((tt, d), lambda n: (prev(n) // n_blocks, 0)),
                  pl.BlockSpec(g.shape, lambda n: (0, 0))],
        out_specs=pl.BlockSpec((tt, d), lambda n: (prev(n) // n_blocks, 0)),
        out_shape=jax.ShapeDtypeStruct((t_total, d), F32),
        scratch_shapes=[pltpu.VMEM((d, tt), F32), pltpu.VMEM((d, tt), BF16),
                        pltpu.VMEM((eblk, tt), F32), pltpu.VMEM((eblk, tt), F32)],
        compiler_params=_params("arbitrary"),
        name="experts",
    )(hn3, u, vt, cnt, ea, rb, eb, h2, g)


def _rope_tables(seq, scale):
    inv = 1.0 / (10000.0 ** (jnp.arange(0, ROPE, 2, dtype=F32) / ROPE))
    ang = jnp.arange(seq, dtype=F32)[:, None] * inv[None, :]
    cos, sin = jnp.cos(ang), jnp.sin(ang)
    half = ROPE // 2
    ones = jnp.ones((seq, NOPE), F32)
    zeros = lambda w: jnp.zeros((seq, w), F32)
    c = jnp.concatenate([ones, cos, cos, zeros(LANES - QK_DIM)], axis=1)
    sa = jnp.concatenate([zeros(NOPE), -sin, zeros(LANES - NOPE - half)], axis=1)
    sb = jnp.concatenate([zeros(NOPE + half), sin, zeros(LANES - QK_DIM)], axis=1)
    return [c * scale, sa * scale, sb * scale]


def _pad_last(a, width):
    return jnp.pad(a, [(0, 0)] * (a.ndim - 1) + [(0, width - a.shape[-1])])


def kernel(x, mem, norm_mix_g, w_in, q_latent_g, w_q_up, kv_latent_g, w_kv_up, conv_w, conv_b,
           conv_ln_g, conv_ln_b, w_mix_out, norm_mem_g, mem_kv_norm_g, w_mem_q, w_mem_kv,
           w_mem_out, norm_ffn_g, w_peer_query, peer_sub_keys, peer_u, peer_v, final_norm_g):
    b, seq, d = x.shape
    assert norm_mix_g.shape[0] == 1, "single-layer block"
    row = lambda a: a.reshape(1, -1).astype(F32)
    x2 = x.reshape(b * seq, d)

    wi = w_in[0]
    win = jnp.concatenate([wi[:, :384], _pad_last(wi[:, 384:416], LANES), wi[:, 416:]], axis=1).astype(BF16)
    wq = _pad_last(w_q_up[0].reshape(Q_RANK, N_HEADS, QK_DIM), LANES).reshape(Q_RANK, -1).astype(BF16)
    wkv = w_kv_up[0].reshape(KV_RANK, N_HEADS, NOPE + V_DIM)
    wk = _pad_last(wkv[:, :, :NOPE], LANES).reshape(KV_RANK, -1).astype(BF16)
    wvt = wkv[:, :, NOPE:].reshape(KV_RANK, -1).T.astype(BF16)
    tables = _rope_tables(seq, QK_DIM ** -0.5 * LOG2_E) + _rope_tables(seq, 1.0)

    qp, kp, vt, uc = _mixer_in(
        x2, row(norm_mix_g[0]), win, row(q_latent_g[0]), wq, row(kv_latent_g[0]), wk, wvt, tables,
        conv_w[0].reshape(CONV_K, CONV_W), row(conv_b[0]), row(conv_ln_g[0]), row(conv_ln_b[0]),
        seq=seq, ts=256)
    width = N_HEADS * LANES
    attn = _attention(qp.reshape(b, seq, width), kp.reshape(b, seq, width), vt,
                      tq=256, tk=256).reshape(b * seq, -1)

    kmem, vmem = _mem_kv(mem, row(mem_kv_norm_g[0]), w_mem_kv[0].astype(BF16))
    h2, hn3, qpeer = _mid(
        x2, attn, uc, w_mix_out[0].astype(BF16), row(norm_mem_g[0]), w_mem_q[0].astype(BF16),
        kmem, vmem, w_mem_out[0].astype(BF16), row(norm_ffn_g[0]), w_peer_query[0].astype(BF16),
        seq=seq, ts=256)

    cnt, ea, rb, eb = _retrieve(qpeer, peer_sub_keys[0].astype(BF16), tr=256)
    out = _experts(hn3, peer_u[0].astype(BF16), peer_v[0].T.astype(BF16), cnt, ea, rb, eb, h2,
                   row(final_norm_g), tt=512, rows_per_step=8, rows_per_chunk=2)
    return out.reshape(b, seq, d)
```
